```python
import math
import functools
import jax
import jax.numpy as jnp
from jax import lax
import numpy as np

D_MODEL = 1024
BATCH = 4
SEQ = 4096
DEPTH = 1
DEC_BATCH = 128
DEC_SEQ = 4
PAST_LEN = 8192
PAGE_SIZE = 128

HEAD_DIM = 64
N_DIFF_HEADS = 4
DIFF_V_DIM = 2 * HEAD_DIM
N_SB_HEADS = 8
SB_V_DIM = HEAD_DIM
DIFF_QK_WIDTH = N_DIFF_HEADS * 2 * HEAD_DIM
DIFF_WIDTH = N_DIFF_HEADS * DIFF_V_DIM
SB_QK_WIDTH = N_SB_HEADS * HEAD_DIM
SB_WIDTH = N_SB_HEADS * SB_V_DIM
MIX_WIDTH = DIFF_WIDTH + SB_WIDTH
QKV_WIDTHS = (DIFF_QK_WIDTH, DIFF_QK_WIDTH, DIFF_WIDTH, SB_QK_WIDTH, SB_QK_WIDTH, SB_WIDTH)
QKV_WIDTH = sum(QKV_WIDTHS)
ROT_DIM = HEAD_DIM // 4
ROPE_THETA = 500000.0
N_MEM = 256
N_MEM_HEADS = 4
MEM_HEAD_DIM = D_MODEL // N_MEM_HEADS
N_EXPERTS = 32
TOP_K = 4
D_EXPERT = D_MODEL
SWIGLU_ALPHA = 1.702
SWIGLU_LIMIT = 7.0
Q_BLOCK = 128
MOE_BLOCK = 128
LN_EPS = 1e-5
RMS_EPS = 1e-6
DEEPNORM_ALPHA = (2.0 * DEPTH) ** 0.25
DEEPNORM_BETA = (8.0 * DEPTH) ** -0.25

kernel_name = 'hybrid_diff_stickbreak_moe_decode_step'


def layer_norm(x, g, b):
    xf = x.astype(jnp.float32)
    mu = jnp.mean(xf, axis=-1, keepdims=True)
    var = jnp.mean(jnp.square(xf - mu), axis=-1, keepdims=True)
    return ((xf - mu) * lax.rsqrt(var + LN_EPS) * g + b).astype(x.dtype)


def rms_heads(o, g):
    of = o.astype(jnp.float32)
    return (of * lax.rsqrt(jnp.mean(of * of, axis=-1, keepdims=True) + RMS_EPS) * g).astype(o.dtype)


def partial_rope(x, pos):
    inv_freq = ROPE_THETA ** (-jnp.arange(0, ROT_DIM, 2, dtype=jnp.float32) / ROT_DIM)
    ang = pos.astype(jnp.float32)[:, None] * inv_freq
    cos = jnp.cos(ang)[:, None, None, :]
    sin = jnp.sin(ang)[:, None, None, :]
    x1 = x[..., : ROT_DIM // 2]
    x2 = x[..., ROT_DIM // 2: ROT_DIM]
    rot = jnp.concatenate([x1 * cos - x2 * sin, x2 * cos + x1 * sin], axis=-1).astype(x.dtype)
    return jnp.concatenate([rot, x[..., ROT_DIM:]], axis=-1)


def project_heads(x, pos, w_in):
    b, t, _ = x.shape
    splits = np.cumsum(QKV_WIDTHS)[:-1].tolist()
    q_d, k_d, v_d, q_s, k_s, v_s = jnp.split(x @ w_in, splits, axis=-1)
    q_d = partial_rope(q_d.reshape(b, t, N_DIFF_HEADS, 2, HEAD_DIM), pos)
    k_d = partial_rope(k_d.reshape(b, t, N_DIFF_HEADS, 2, HEAD_DIM), pos).reshape(b, t, N_DIFF_HEADS, 2 * HEAD_DIM)
    v_d = v_d.reshape(b, t, N_DIFF_HEADS, DIFF_V_DIM)
    q_s = q_s.reshape(b, t, N_SB_HEADS, HEAD_DIM)
    k_s = k_s.reshape(b, t, N_SB_HEADS, HEAD_DIM)
    v_s = v_s.reshape(b, t, N_SB_HEADS, SB_V_DIM)
    return q_d, k_d, v_d, q_s, k_s, v_s


def diff_attention(q, k, v, q_pos, k_pos, lam):
    s = jnp.einsum('bqhcd,bkhcd->bhcqk', q, k, preferred_element_type=jnp.float32) * (HEAD_DIM ** -0.5)
    causal = k_pos[None, :] <= q_pos[:, None]
    p = jax.nn.softmax(jnp.where(causal, s, -jnp.inf), axis=-1)
    a = p[:, :, 0] - lam * p[:, :, 1]
    return jnp.einsum('bhqk,bkhd->bqhd', a.astype(v.dtype), v)


def stick_breaking(q, k, v, q_pos, k_pos):
    z = jnp.einsum('bqhd,bkhd->bhqk', q, k, preferred_element_type=jnp.float32) * (HEAD_DIM ** -0.5)
    visible = k_pos[None, :] < q_pos[:, None]
    log_not = jnp.where(visible, jax.nn.log_sigmoid(-z), 0.0)
    suffix = lax.cumsum(log_not, axis=3, reverse=True)
    log_stay = jnp.concatenate([suffix[..., 1:], jnp.zeros_like(suffix[..., :1])], axis=-1)
    a = jnp.where(visible, jnp.exp(jax.nn.log_sigmoid(z) + log_stay), 0.0)
    return jnp.einsum('bhqk,bkhd->bqhd', a.astype(v.dtype), v)


def prompt_mix(q_d, k_d, v_d, q_s, k_s, v_s, lam):
    b, t = q_d.shape[:2]
    n_blocks = t // Q_BLOCK
    k_pos = jnp.arange(t)
    k_d5 = k_d.reshape(b, t, N_DIFF_HEADS, 2, HEAD_DIM)

    def block(i):
        start = i * Q_BLOCK
        q_pos = start + jnp.arange(Q_BLOCK)
        qd = lax.dynamic_slice_in_dim(q_d, start, Q_BLOCK, axis=1)
        qs = lax.dynamic_slice_in_dim(q_s, start, Q_BLOCK, axis=1)
        return (diff_attention(qd, k_d5, v_d, q_pos, k_pos, lam),
                stick_breaking(qs, k_s, v_s, q_pos, k_pos))

    od, os_ = lax.map(block, jnp.arange(n_blocks))
    od = jnp.moveaxis(od, 0, 1).reshape(b, t, N_DIFF_HEADS, DIFF_V_DIM)
    os_ = jnp.moveaxis(os_, 0, 1).reshape(b, t, N_SB_HEADS, SB_V_DIM)
    return od, os_


def sample_mix(q_d, k_d, v_d, q_s, k_s, v_s, lam, cache_dk, cache_dv, cache_sk, cache_sv, page_table, layer):
    t = q_d.shape[1]
    past = page_table.shape[1] * PAGE_SIZE
    k_pos = jnp.arange(past + t)
    q_pos = past + jnp.arange(t)

    def gather(cache, pages, new):
        rows = cache[layer, pages].reshape((past,) + cache.shape[3:])
        return jnp.concatenate([rows, new], axis=0)[None]

    def one(args):
        qd, kd, vd, qs, ks, vs, pages = args
        kd_all = gather(cache_dk, pages, kd).reshape(1, past + t, N_DIFF_HEADS, 2, HEAD_DIM)
        vd_all = gather(cache_dv, pages, vd)
        ks_all = gather(cache_sk, pages, ks)
        vs_all = gather(cache_sv, pages, vs)
        od = diff_attention(qd[None], kd_all, vd_all, q_pos, k_pos, lam)[0]
        os_ = stick_breaking(qs[None], ks_all, vs_all, q_pos, k_pos)[0]
        return od, os_

    return lax.map(one, (q_d, k_d, v_d, q_s, k_s, v_s, page_table))


def merge_heads(od, os_, g_diff, g_sb, w_out, lambda_init):
    b, t = od.shape[:2]
    od = rms_heads(od, g_diff.reshape(N_DIFF_HEADS, DIFF_V_DIM)) * (1.0 - lambda_init)
    os_ = rms_heads(os_, g_sb.reshape(N_SB_HEADS, SB_V_DIM))
    return jnp.concatenate([od.reshape(b, t, DIFF_WIDTH), os_.reshape(b, t, SB_WIDTH)], axis=-1) @ w_out


def mem_kv(mem, w_ck, w_cv):
    b = mem.shape[0]
    mk = (mem @ w_ck).reshape(b, N_MEM, N_MEM_HEADS, MEM_HEAD_DIM)
    mv = (mem @ w_cv).reshape(b, N_MEM, N_MEM_HEADS, MEM_HEAD_DIM)
    return mk, mv


def mem_attend(x, mk, mv, w_cq, w_co):
    b, t, _ = x.shape
    q = (x @ w_cq).reshape(b, t, N_MEM_HEADS, MEM_HEAD_DIM)
    s = jnp.einsum('bqhd,bkhd->bhqk', q, mk, preferred_element_type=jnp.float32) * (MEM_HEAD_DIM ** -0.5)
    p = jax.nn.softmax(s, axis=-1)
    o = jnp.einsum('bhqk,bkhd->bqhd', p.astype(mv.dtype), mv).reshape(b, t, D_MODEL)
    return o @ w_co


def moe(x, w_router, b_router, w_gate, b_gate, w_up, b_up, w_down, b_down):
    b, t, d = x.shape
    xt = x.reshape(-1, d)
    n_tok = xt.shape[0]
    logits = jnp.dot(xt, w_router, preferred_element_type=jnp.float32) + b_router
    top_logit, top_e = lax.top_k(logits, TOP_K)
    gate = jax.nn.softmax(top_logit, axis=-1)
    n_assign = n_tok * TOP_K
    flat_e = top_e.reshape(-1)
    flat_tok = jnp.repeat(jnp.arange(n_tok, dtype=jnp.int32), TOP_K)
    flat_gate = gate.reshape(-1)
    order = jnp.argsort(flat_e)
    e_sorted = flat_e[order]
    tok_sorted = flat_tok[order]
    gate_sorted = flat_gate[order]
    counts = jnp.bincount(flat_e, length=N_EXPERTS)
    padded = (counts + MOE_BLOCK - 1) // MOE_BLOCK * MOE_BLOCK
    pad_end = jnp.cumsum(padded)
    pad_start = pad_end - padded
    start = jnp.cumsum(counts) - counts
    dest = pad_start[e_sorted] + jnp.arange(n_assign) - start[e_sorted]
    n_blocks = (n_assign + N_EXPERTS * (MOE_BLOCK - 1) + MOE_BLOCK - 1) // MOE_BLOCK
    n_rows = n_blocks * MOE_BLOCK
    row_tok = jnp.zeros((n_rows,), jnp.int32).at[dest].set(tok_sorted)
    block_e = jnp.minimum(jnp.searchsorted(pad_end, jnp.arange(n_blocks) * MOE_BLOCK, side='right'), N_EXPERTS - 1)
    xb = xt[row_tok].reshape(n_blocks, MOE_BLOCK, d)

    def expert_block(args):
        xe, e = args
        g = jnp.minimum(xe @ w_gate[e] + b_gate[e], SWIGLU_LIMIT)
        u = jnp.clip(xe @ w_up[e] + b_up[e], -SWIGLU_LIMIT, SWIGLU_LIMIT)
        h = (u + 1.0) * g * jax.nn.sigmoid(SWIGLU_ALPHA * g)
        return h @ w_down[e] + b_down[e]

    yb = lax.map(expert_block, (xb, block_e)).reshape(n_rows, d)
    y = jnp.zeros_like(xt).at[tok_sorted].add(gate_sorted[:, None].astype(x.dtype) * yb[dest])
    return y.reshape(b, t, d)


def decoder_layer(x, pos, mix_fn, mem_k, mem_v, lp, lambda_init):
    (w_in, lq1, lk1, lq2, lk2, g_diff, g_sb, w_out, ln1_g, ln1_b, w_cq, w_co, ln2_g, ln2_b,
     w_router, b_router, w_gate, b_gate, w_up, b_up, w_down, b_down, ln3_g, ln3_b) = lp
    lam = (jnp.exp(jnp.sum(lq1.astype(jnp.float32) * lk1.astype(jnp.float32)))
           - jnp.exp(jnp.sum(lq2.astype(jnp.float32) * lk2.astype(jnp.float32))) + lambda_init)
    q_d, k_d, v_d, q_s, k_s, v_s = project_heads(x, pos, w_in)
    od, os_ = mix_fn(q_d, k_d, v_d, q_s, k_s, v_s, lam)
    x = layer_norm(DEEPNORM_ALPHA * x + merge_heads(od, os_, g_diff, g_sb, w_out, lambda_init), ln1_g, ln1_b)
    x = layer_norm(DEEPNORM_ALPHA * x + mem_attend(x, mem_k, mem_v, w_cq, w_co), ln2_g, ln2_b)
    x = layer_norm(DEEPNORM_ALPHA * x + moe(x, w_router, b_router, w_gate, b_gate, w_up, b_up, w_down, b_down), ln3_g, ln3_b)
    return x, (k_d, v_d, k_s, v_s)


def setup_inputs(seed: int = 0) -> dict:
    key = jax.random.key(seed)
    ks = jax.random.split(key, 40)

    def nrm(k, shape, scale):
        return jax.random.normal(k, shape, jnp.float32) * scale

    L = DEPTH
    d, e, f = D_MODEL, N_EXPERTS, D_EXPERT
    beta = DEEPNORM_BETA
    n_pages = PAST_LEN // PAGE_SIZE
    n_phys = (DEC_BATCH * n_pages * 5) // 4
    col_scale = jnp.concatenate([jnp.full((w,), s, jnp.float32)
                                 for w, s in zip(QKV_WIDTHS, (1.0, 1.0, beta, 1.0, 1.0, beta))])
    page_table = jax.random.permutation(ks[9], n_phys)[: DEC_BATCH * n_pages].reshape(DEC_BATCH, n_pages).astype(jnp.int32)
    return {
        'x_prompt': nrm(ks[0], (BATCH, SEQ, d), 1.0),
        'x_sample': nrm(ks[1], (DEC_BATCH, DEC_SEQ, d), 1.0),
        'mem_prompt': nrm(ks[2], (BATCH, N_MEM, d), 1.0),
        'cache_diff_k': nrm(ks[3], (L, n_phys, PAGE_SIZE, N_DIFF_HEADS, 2 * HEAD_DIM), 1.0),
        'cache_diff_v': nrm(ks[4], (L, n_phys, PAGE_SIZE, N_DIFF_HEADS, DIFF_V_DIM), beta),
        'cache_sb_k': nrm(ks[5], (L, n_phys, PAGE_SIZE, N_SB_HEADS, HEAD_DIM), 1.0),
        'cache_sb_v': nrm(ks[6], (L, n_phys, PAGE_SIZE, N_SB_HEADS, SB_V_DIM), beta),
        'cache_mem_k': nrm(ks[7], (L, DEC_BATCH, N_MEM, N_MEM_HEADS, MEM_HEAD_DIM), 1.0),
        'cache_mem_v': nrm(ks[8], (L, DEC_BATCH, N_MEM, N_MEM_HEADS, MEM_HEAD_DIM), beta),
        'page_table': page_table,
        'w_in': nrm(ks[10], (L, d, QKV_WIDTH), d ** -0.5) * col_scale,
        'lambda_q1': nrm(ks[11], (L, HEAD_DIM), 0.1),
        'lambda_k1': nrm(ks[12], (L, HEAD_DIM), 0.1),
        'lambda_q2': nrm(ks[13], (L, HEAD_DIM), 0.1),
        'lambda_k2': nrm(ks[14], (L, HEAD_DIM), 0.1),
        'g_diff': 1.0 + nrm(ks[15], (L, DIFF_WIDTH), 0.02),
        'g_sb': 1.0 + nrm(ks[16], (L, SB_WIDTH), 0.02),
        'w_out': nrm(ks[17], (L, MIX_WIDTH, d), MIX_WIDTH ** -0.5 * beta),
        'ln1_g': 1.0 + nrm(ks[18], (L, d), 0.02),
        'ln1_b': nrm(ks[19], (L, d), 0.02),
        'w_cq': nrm(ks[20], (L, d, d), d ** -0.5),
        'w_ck': nrm(ks[21], (L, d, d), d ** -0.5),
        'w_cv': nrm(ks[22], (L, d, d), d ** -0.5 * beta),
        'w_co': nrm(ks[23], (L, d, d), d ** -0.5 * beta),
        'ln2_g': 1.0 + nrm(ks[24], (L, d), 0.02),
        'ln2_b': nrm(ks[25], (L, d), 0.02),
        'w_router': nrm(ks[26], (L, d, e), d ** -0.5),
        'b_router': nrm(ks[27], (L, e), 0.01),
        'w_gate': nrm(ks[28], (L, e, d, f), d ** -0.5),
        'b_gate': nrm(ks[29], (L, e, f), 0.02),
        'w_up': nrm(ks[30], (L, e, d, f), d ** -0.5),
        'b_up': nrm(ks[31], (L, e, f), 0.02),
        'w_down': nrm(ks[32], (L, e, f, d), f ** -0.5 * beta),
        'b_down': nrm(ks[33], (L, e, d), 0.02),
        'ln3_g': 1.0 + nrm(ks[34], (L, d), 0.02),
        'ln3_b': nrm(ks[35], (L, d), 0.02),
    }


def reference(x_prompt, x_sample, mem_prompt, cache_diff_k, cache_diff_v, cache_sb_k, cache_sb_v,
              cache_mem_k, cache_mem_v, page_table, w_in, lambda_q1, lambda_k1, lambda_q2, lambda_k2,
              g_diff, g_sb, w_out, ln1_g, ln1_b, w_cq, w_ck, w_cv, w_co, ln2_g, ln2_b,
              w_router, b_router, w_gate, b_gate, w_up, b_up, w_down, b_down, ln3_g, ln3_b):
    y_p, y_s = x_prompt, x_sample
    past = page_table.shape[1] * PAGE_SIZE
    pos_p = jnp.arange(x_prompt.shape[1])
    pos_s = past + jnp.arange(x_sample.shape[1])
    new_state = [[] for _ in range(10)]
    for l in range(DEPTH):
        lambda_init = 0.8 - 0.6 * math.exp(-0.3 * l)
        lp = (w_in[l], lambda_q1[l], lambda_k1[l], lambda_q2[l], lambda_k2[l], g_diff[l], g_sb[l], w_out[l],
              ln1_g[l], ln1_b[l], w_cq[l], w_co[l], ln2_g[l], ln2_b[l], w_router[l], b_router[l],
              w_gate[l], b_gate[l], w_up[l], b_up[l], w_down[l], b_down[l], ln3_g[l], ln3_b[l])
        mk_p, mv_p = mem_kv(mem_prompt, w_ck[l], w_cv[l])
        y_p, (dk_p, dv_p, sk_p, sv_p) = decoder_layer(y_p, pos_p, prompt_mix, mk_p, mv_p, lp, lambda_init)
        mix_s = functools.partial(sample_mix, cache_dk=cache_diff_k, cache_dv=cache_diff_v,
                                  cache_sk=cache_sb_k, cache_sv=cache_sb_v, page_table=page_table, layer=l)
        y_s, (dk_s, dv_s, sk_s, sv_s) = decoder_layer(y_s, pos_s, mix_s, cache_mem_k[l], cache_mem_v[l], lp, lambda_init)
        for i, a in enumerate((dk_p, dv_p, sk_p, sv_p, mk_p, mv_p, dk_s, dv_s, sk_s, sv_s)):
            new_state[i].append(a)
    (p_dk, p_dv, p_sk, p_sv, p_mk, p_mv, s_dk, s_dv, s_sk, s_sv) = [jnp.stack(a) for a in new_state]
    return (y_p, y_s, p_dk, p_dv, p_sk, p_sv, p_mk, p_mv, s_dk, s_dv, s_sk, s_sv)
```

```python
import functools
import math

import jax
import jax.numpy as jnp
import numpy as np
from jax import lax
from jax.experimental import pallas as pl
from jax.experimental.pallas import tpu as pltpu

F32 = jnp.float32
BF16 = jnp.bfloat16

D_MODEL = 1024
HEAD_DIM = 64
N_DIFF_HEADS = 4
N_SB_HEADS = 8
GROUP = 512
LANES = 128
ROT_DIM = HEAD_DIM // 4
ROPE_THETA = 500000.0
PAGE_SIZE = 128
N_MEM_HEADS = 4
MEM_HEAD_DIM = D_MODEL // N_MEM_HEADS
N_EXPERTS = 32
TOP_K = 4
SWIGLU_ALPHA = 1.702
SWIGLU_LIMIT = 7.0
LN_EPS = 1e-5
RMS_EPS = 1e-6
DEPTH = 1
DEEPNORM_ALPHA = (2.0 * DEPTH) ** 0.25
QK_SCALE = HEAD_DIM ** -0.5
NEG_BIG = -1e30
SB_FLOOR = -104.0
VMEM_LIMIT = 48 * 1024 * 1024


def _params(*sem):
    return pltpu.CompilerParams(dimension_semantics=sem, vmem_limit_bytes=VMEM_LIMIT)


def _rope_tables(pos):
    inv_freq = ROPE_THETA ** (-jnp.arange(0, ROT_DIM, 2, dtype=F32) / ROT_DIM)
    ang = pos.astype(F32)[:, None] * inv_freq
    cos, sin = jnp.cos(ang), jnp.sin(ang)
    n = pos.shape[0]
    half = ROT_DIM // 2
    pad = HEAD_DIM - ROT_DIM
    c = jnp.concatenate([cos, cos, jnp.ones((n, pad), F32)], axis=1)
    s_lo = jnp.concatenate([-sin, jnp.zeros((n, HEAD_DIM - half), F32)], axis=1)
    s_hi = jnp.concatenate([jnp.zeros((n, half), F32), sin, jnp.zeros((n, pad), F32)], axis=1)
    rep = LANES // HEAD_DIM
    return tuple(jnp.tile(t, (1, rep)) for t in (c, s_lo, s_hi))


def _rope(x, c, s_lo, s_hi):
    half = ROT_DIM // 2
    outs = []
    for j in range(x.shape[1] // LANES):
        xj = x[:, j * LANES:(j + 1) * LANES]
        outs.append(xj * c + pltpu.roll(xj, LANES - half, 1) * s_lo + pltpu.roll(xj, half, 1) * s_hi)
    return jnp.concatenate(outs, axis=1)


def _expand_heads(q, width):
    lane = lax.broadcasted_iota(jnp.int32, (q.shape[0], LANES), 1)
    per = LANES // width
    outs = []
    for j in range(q.shape[1] // LANES):
        qj = q[:, j * LANES:(j + 1) * LANES]
        for i in range(per):
            keep = (lane >= i * width) & (lane < (i + 1) * width)
            outs.append(jnp.where(keep, qj, 0.0))
    return jnp.concatenate(outs, axis=1)


def _proj_kernel(x_ref, w_ref, wt_ref, c_ref, slo_ref, shi_ref,
                 qd_ref, qdp_ref, kd_ref, kdb_ref, vd_ref, vdb_ref, qs_ref, qsp_ref,
                 kst_ref, kstb_ref, vst_ref, vstb_ref):
    x = x_ref[0].astype(BF16)
    tm = x.shape[0]
    c, s_lo, s_hi = c_ref[...], slo_ref[...], shi_ref[...]

    def seg(i):
        return jnp.dot(x, w_ref[:, i * GROUP:(i + 1) * GROUP], preferred_element_type=F32)

    def store_by_head(ref, val):
        for h in range(N_DIFF_HEADS):
            ref[pl.ds(h, tm, stride=N_DIFF_HEADS), :] = val[:, h * LANES:(h + 1) * LANES]

    qd = _rope(seg(0), c, s_lo, s_hi) * QK_SCALE
    qd_ref[0] = _expand_heads(qd, HEAD_DIM).astype(BF16)
    qdp_ref[0] = qd.astype(BF16)
    kd = _rope(seg(1), c, s_lo, s_hi)
    store_by_head(kd_ref, kd)
    kdb_ref[0] = kd.astype(BF16)
    vd = seg(2)
    store_by_head(vd_ref, vd)
    vdb_ref[0] = vd.astype(BF16)
    q_s = seg(3) * QK_SCALE
    qs_ref[0] = _expand_heads(q_s, HEAD_DIM).astype(BF16)
    qsp_ref[0] = q_s.astype(BF16)
    kv_t = _dot_nt(wt_ref[...], x)
    kst_ref[0] = kv_t[0:GROUP]
    kstb_ref[0] = kv_t[0:GROUP].astype(BF16)
    vst_ref[0] = kv_t[GROUP:2 * GROUP]
    vstb_ref[0] = kv_t[GROUP:2 * GROUP].astype(BF16)


def _project(x, pos, w_in):
    b, t, d = x.shape
    tm = min(512, t)
    assert t % tm == 0
    nt = t // tm
    w_cols = w_in[:, 0:4 * GROUP].astype(BF16)
    w_rows = w_in[:, 4 * GROUP:6 * GROUP].T.astype(BF16)
    c, s_lo, s_hi = _rope_tables(pos)
    tok = lambda w: pl.BlockSpec((1, tm, w), lambda bi, i: (bi, i, 0))
    tab = pl.BlockSpec((tm, LANES), lambda bi, i: (bi * nt + i, 0))
    by_head = pl.BlockSpec((tm * N_DIFF_HEADS, LANES), lambda bi, i: (bi * nt + i, 0))
    tr = pl.BlockSpec((1, GROUP, tm), lambda bi, i: (bi, 0, i))
    whole = lambda a: pl.BlockSpec(a.shape, lambda bi, i: (0, 0))
    sds = jax.ShapeDtypeStruct
    q_exp, plain = sds((b, t, 2 * GROUP), BF16), sds((b, t, GROUP), BF16)
    rows = sds((b * t * N_DIFF_HEADS, LANES), F32)
    tr32, tr16 = sds((b, GROUP, t), F32), sds((b, GROUP, t), BF16)
    return pl.pallas_call(
        _proj_kernel,
        grid=(b, nt),
        in_specs=[tok(d), whole(w_cols), whole(w_rows), tab, tab, tab],
        out_specs=[tok(2 * GROUP), tok(GROUP), by_head, tok(GROUP), by_head, tok(GROUP),
                   tok(2 * GROUP), tok(GROUP), tr, tr, tr, tr],
        out_shape=[q_exp, plain, rows, plain, rows, plain, q_exp, plain, tr32, tr16, tr32, tr16],
        compiler_params=_params("parallel", "parallel"),
        name="qkv_proj",
    )(x, w_cols, w_rows, c, s_lo, s_hi)


def _dot_nt(a, b):
    return lax.dot_general(a, b, (((1,), (1,)), ((), ())), preferred_element_type=F32)


def _softplus(z):
    return jnp.maximum(z, 0.0) + jnp.log1p(jnp.exp(-jnp.abs(z)))


def _split_bf16(x):
    hi = x.astype(BF16)
    lo = (x - hi.astype(F32)).astype(BF16)
    return hi, lo


def _lambda(lq1, lk1, lq2, lk2, lambda_init):
    return (jnp.exp(jnp.sum(lq1 * lk1, axis=1, keepdims=True))
            - jnp.exp(jnp.sum(lq2 * lk2, axis=1, keepdims=True)) + lambda_init)


def _prompt_attn_kernel(qi_ref, kj_ref, qd_ref, kd_ref, vd_ref, qs_ref, kst_ref, vst_ref,
                        lq1_ref, lk1_ref, lq2_ref, lk2_ref, od_ref, os_ref,
                        m_scr, l_scr, accd_scr, accs_scr, run_scr, *, blk, lambda_init):
    step = pl.program_id(1)
    qi = qi_ref[step]
    kj = kj_ref[step]
    n_streams = 2 * N_DIFF_HEADS

    @pl.when(kj == qi)
    def _():
        m_scr[...] = jnp.full(m_scr.shape, NEG_BIG, F32)
        l_scr[...] = jnp.zeros(l_scr.shape, F32)
        accd_scr[...] = jnp.zeros(accd_scr.shape, F32)
        accs_scr[...] = jnp.zeros(accs_scr.shape, F32)
        run_scr[...] = jnp.zeros(run_scr.shape, F32)

    row = qi * blk + lax.broadcasted_iota(jnp.int32, (blk, blk), 0)
    col = kj * blk + lax.broadcasted_iota(jnp.int32, (blk, blk), 1)
    causal = col <= row
    strict = col < row

    for s in range(n_streams):
        h = s // 2
        q = qd_ref[0, :, s * LANES:(s + 1) * LANES]
        k = kd_ref[0, :, h * LANES:(h + 1) * LANES]
        sc = jnp.where(causal, _dot_nt(q, k), NEG_BIG)
        m_old = m_scr[s]
        m_new = jnp.maximum(m_old, jnp.max(sc, axis=1, keepdims=True))
        p = jnp.exp(sc - m_new)
        alpha = jnp.exp(m_old - m_new)
        l_scr[s] = alpha * l_scr[s] + jnp.sum(p, axis=1, keepdims=True)
        m_scr[s] = m_new
        v = vd_ref[0, :, h * LANES:(h + 1) * LANES]
        accd_scr[s] = alpha * accd_scr[s] + jnp.dot(p.astype(BF16), v, preferred_element_type=F32)

    tri = (lax.broadcasted_iota(jnp.int32, (blk, blk), 0)
           > lax.broadcasted_iota(jnp.int32, (blk, blk), 1)).astype(BF16)
    def sb_head(h):
        g = h // 2
        q = qs_ref[0, :, h * LANES:(h + 1) * LANES]
        z = jnp.dot(q, kst_ref[0, g * LANES:(g + 1) * LANES, :], preferred_element_type=F32)
        log_not = jnp.where(strict, -_softplus(z), 0.0)
        hi, lo = _split_bf16(log_not)
        suffix = (jnp.dot(hi, tri, preferred_element_type=F32)
                  + jnp.dot(lo, tri, preferred_element_type=F32))
        run = run_scr[h]
        a = jnp.where(strict, jnp.exp(z + log_not + suffix + run), 0.0)
        run_scr[h] = run + jnp.sum(log_not, axis=1, keepdims=True)
        accs_scr[h] += _dot_nt(a.astype(BF16), vst_ref[0, g * LANES:(g + 1) * LANES, :])

    for h in range(N_SB_HEADS):
        pl.when(jnp.max(run_scr[h]) > SB_FLOOR)(functools.partial(sb_head, h))

    @pl.when(kj == 0)
    def _():
        lam = _lambda(lq1_ref[...], lk1_ref[...], lq2_ref[...], lk2_ref[...], lambda_init)
        for h in range(N_DIFF_HEADS):
            o0 = accd_scr[2 * h] / l_scr[2 * h]
            o1 = accd_scr[2 * h + 1] / l_scr[2 * h + 1]
            od_ref[0, :, h * LANES:(h + 1) * LANES] = o0 - lam * o1
        lane = lax.broadcasted_iota(jnp.int32, (blk, LANES), 1)
        for g in range(N_SB_HEADS // 2):
            os_ref[0, :, g * LANES:(g + 1) * LANES] = jnp.where(
                lane < HEAD_DIM, accs_scr[2 * g], accs_scr[2 * g + 1])


def _prompt_attention(qd, kd, vd, qs, kst, vst, lams, lambda_init, blk=256):
    b, t, _ = kd.shape
    blk = min(blk, t)
    nq = t // blk
    pairs = [(i, j) for i in range(nq) for j in range(i, -1, -1)]
    qi_tab = jnp.asarray(np.array([p[0] for p in pairs], np.int32))
    kj_tab = jnp.asarray(np.array([p[1] for p in pairs], np.int32))
    qspec = pl.BlockSpec((1, blk, 2 * GROUP), lambda bi, s, qi, kj: (bi, qi[s], 0))
    kspec = pl.BlockSpec((1, blk, GROUP), lambda bi, s, qi, kj: (bi, kj[s], 0))
    tspec = pl.BlockSpec((1, GROUP, blk), lambda bi, s, qi, kj: (bi, 0, kj[s]))
    ospec = pl.BlockSpec((1, blk, GROUP), lambda bi, s, qi, kj: (bi, qi[s], 0))
    lspec = pl.BlockSpec((1, HEAD_DIM), lambda bi, s, qi, kj: (0, 0))
    out = jax.ShapeDtypeStruct((b, t, GROUP), F32)
    return pl.pallas_call(
        functools.partial(_prompt_attn_kernel, blk=blk, lambda_init=lambda_init),
        grid_spec=pltpu.PrefetchScalarGridSpec(
            num_scalar_prefetch=2,
            grid=(b, len(pairs)),
            in_specs=[qspec, kspec, kspec, qspec, tspec, tspec, lspec, lspec, lspec, lspec],
            out_specs=[ospec, ospec],
            scratch_shapes=[pltpu.VMEM((2 * N_DIFF_HEADS, blk, 1), F32),
                            pltpu.VMEM((2 * N_DIFF_HEADS, blk, 1), F32),
                            pltpu.VMEM((2 * N_DIFF_HEADS, blk, LANES), F32),
                            pltpu.VMEM((N_SB_HEADS, blk, LANES), F32),
                            pltpu.VMEM((N_SB_HEADS, blk, 1), F32)]),
        out_shape=[out, out],
        compiler_params=_params("parallel", "arbitrary"),
        name="prompt_attn",
    )(qi_tab, kj_tab, qd, kd, vd, qs, kst, vst, *lams)


NEW_PAD = PAGE_SIZE
SUBLANES = 8


def _sample_attn_kernel(pt_ref, qd_ref, qs_ref, knd_ref, vnd_ref, kns_ref, vns_ref,
                        lq1_ref, lk1_ref, lq2_ref, lk2_ref, *rest, pages_per_step, t_new, lambda_init):
    del pt_ref
    P = pages_per_step
    kd_pages, vd_pages = rest[0:P], rest[P:2 * P]
    ks_pages, vs_pages = rest[2 * P:3 * P], rest[3 * P:4 * P]
    od_ref, os_ref, m_scr, l_scr, accd_scr, run_scr, accs_scr = rest[4 * P:]
    j = pl.program_id(1)

    @pl.when(j == 0)
    def _():
        m_scr[...] = jnp.full(m_scr.shape, NEG_BIG, F32)
        l_scr[...] = jnp.zeros(l_scr.shape, F32)
        accd_scr[...] = jnp.zeros(accd_scr.shape, F32)
        accs_scr[...] = jnp.zeros(accs_scr.shape, F32)
        run_scr[...] = jnp.zeros(run_scr.shape, F32)

    def diff_block(k_refs, v_refs, is_new):
        ntok = k_refs[0].shape[0] // N_DIFF_HEADS

        def head_rows(refs, h):
            rows = [r[pl.ds(h, ntok, stride=N_DIFF_HEADS), :].astype(BF16) for r in refs]
            return rows[0] if len(rows) == 1 else jnp.concatenate(rows, axis=0)

        for h in range(N_DIFF_HEADS):
            k = head_rows(k_refs, h)
            v = head_rows(v_refs, h)
            s = _dot_nt(qd_ref[0, h].astype(BF16), k)
            if is_new:
                tq = lax.broadcasted_iota(jnp.int32, s.shape, 0) % t_new
                tk = lax.broadcasted_iota(jnp.int32, s.shape, 1)
                s = jnp.where(tk <= tq, s, NEG_BIG)
            m_old = m_scr[h]
            m_new = jnp.maximum(m_old, jnp.max(s, axis=1, keepdims=True))
            p = jnp.exp(s - m_new)
            alpha = jnp.exp(m_old - m_new)
            l_scr[h] = alpha * l_scr[h] + jnp.sum(p, axis=1, keepdims=True)
            m_scr[h] = m_new
            accd_scr[h] = alpha * accd_scr[h] + jnp.dot(p.astype(BF16), v, preferred_element_type=F32)

    def sb_block(kt_refs, vt_refs, is_new):
        ntok = kt_refs[0].shape[-1]
        n = len(kt_refs)
        tri = (lax.broadcasted_iota(jnp.int32, (ntok, ntok), 0)
               > lax.broadcasted_iota(jnp.int32, (ntok, ntok), 1)).astype(BF16)

        def head_cols(refs, h):
            cols = [r[h].astype(BF16) for r in refs]
            return cols[0] if n == 1 else jnp.concatenate(cols, axis=1)

        for h in range(N_SB_HEADS):
            z = jnp.dot(qs_ref[0, h].astype(BF16), head_cols(kt_refs, h), preferred_element_type=F32)
            log_not = -_softplus(z)
            if is_new:
                tq = lax.broadcasted_iota(jnp.int32, z.shape, 0)
                tk = lax.broadcasted_iota(jnp.int32, z.shape, 1)
                visible = tk < tq
                log_not = jnp.where(visible, log_not, 0.0)
            chunk = lambda x, c: x[:, c * ntok:(c + 1) * ntok]
            stacked = log_not if n == 1 else jnp.concatenate([chunk(log_not, c) for c in range(n)], axis=0)
            hi, lo = _split_bf16(stacked)
            local = (jnp.dot(hi, tri, preferred_element_type=F32)
                     + jnp.dot(lo, tri, preferred_element_type=F32))
            offset = run_scr[h]
            pieces = []
            for c in range(n):
                suffix = local[c * SUBLANES:(c + 1) * SUBLANES]
                pieces.append(jnp.exp(chunk(z, c) + chunk(log_not, c) + suffix + offset))
                offset = offset + jnp.sum(chunk(log_not, c), axis=1, keepdims=True)
            run_scr[h] = offset
            a = pieces[0] if n == 1 else jnp.concatenate(pieces, axis=1)
            if is_new:
                a = jnp.where(visible, a, 0.0)
            accs_scr[h] += _dot_nt(a.astype(BF16), head_cols(vt_refs, h))

    @pl.when(j == 0)
    def _():
        diff_block([knd_ref.at[0]], [vnd_ref.at[0]], True)
        sb_block([kns_ref.at[0]], [vns_ref.at[0]], True)

    diff_block(kd_pages, vd_pages, False)
    pl.when(jnp.max(run_scr[:, 0:t_new, :]) > SB_FLOOR)(
        functools.partial(sb_block, ks_pages, vs_pages, False))

    @pl.when(j == pl.num_programs(1) - 1)
    def _():
        lam = _lambda(lq1_ref[...], lk1_ref[...], lq2_ref[...], lk2_ref[...], lambda_init)
        for h in range(N_DIFF_HEADS):
            o = accd_scr[h] / l_scr[h]
            od_ref[0, h] = o[0:t_new] - lam * o[t_new:2 * t_new]
        os_ref[0] = accs_scr[...]


def _sample_attention(qd_rows, qs_rows, knd, vnd, kns, vns, lams, lambda_init, t_new,
                      cache_dk, cache_dv, cache_sk, cache_sv, page_table, layer, pages_per_step=8):
    n_seq, n_pages = page_table.shape
    assert 2 * t_new == SUBLANES
    P = min(pages_per_step, n_pages)
    assert n_pages % P == 0
    n_steps = n_pages // P
    seq3 = lambda a: pl.BlockSpec((1,) + a.shape[1:], lambda s, j, pt: (s,) + (0,) * (a.ndim - 1))
    lspec = pl.BlockSpec((1, HEAD_DIM), lambda s, j, pt: (0, 0))
    n_phys = cache_dk.shape[1]
    dk = cache_dk.reshape(cache_dk.shape[0], n_phys, PAGE_SIZE * N_DIFF_HEADS, LANES)
    dv = cache_dv.reshape(cache_dv.shape[0], n_phys, PAGE_SIZE * N_DIFF_HEADS, LANES)
    sk = cache_sk.transpose(0, 1, 3, 4, 2)
    sv = cache_sv.transpose(0, 1, 3, 4, 2)

    def page_spec(cache, p):
        blk = (None, None) + cache.shape[2:]
        tail = (0,) * (cache.ndim - 2)
        return pl.BlockSpec(blk, lambda s, j, pt: (layer, pt[s, n_pages - 1 - (j * P + p)]) + tail)

    caches = (dk, dv, sk, sv)
    page_specs = [page_spec(c, p) for c in caches for p in range(P)]
    page_args = [c for c in caches for p in range(P)]
    od_shape = jax.ShapeDtypeStruct((n_seq, N_DIFF_HEADS, t_new, LANES), F32)
    os_shape = jax.ShapeDtypeStruct((n_seq, N_SB_HEADS, SUBLANES, HEAD_DIM), F32)
    return pl.pallas_call(
        functools.partial(_sample_attn_kernel, pages_per_step=P, t_new=t_new, lambda_init=lambda_init),
        grid_spec=pltpu.PrefetchScalarGridSpec(
            num_scalar_prefetch=1,
            grid=(n_seq, n_steps),
            in_specs=[seq3(qd_rows), seq3(qs_rows), seq3(knd), seq3(vnd), seq3(kns), seq3(vns),
                      lspec, lspec, lspec, lspec] + page_specs,
            out_specs=[pl.BlockSpec((1, N_DIFF_HEADS, t_new, LANES), lambda s, j, pt: (s, 0, 0, 0)),
                       pl.BlockSpec((1, N_SB_HEADS, SUBLANES, HEAD_DIM), lambda s, j, pt: (s, 0, 0, 0))],
            scratch_shapes=[pltpu.VMEM((N_DIFF_HEADS, SUBLANES, 1), F32),
                            pltpu.VMEM((N_DIFF_HEADS, SUBLANES, 1), F32),
                            pltpu.VMEM((N_DIFF_HEADS, SUBLANES, LANES), F32),
                            pltpu.VMEM((N_SB_HEADS, SUBLANES, 1), F32),
                            pltpu.VMEM((N_SB_HEADS, SUBLANES, HEAD_DIM), F32)]),
        out_shape=[od_shape, os_shape],
        compiler_params=_params("parallel", "arbitrary"),
        name="sample_attn",
    )(page_table, qd_rows, qs_rows, knd, vnd, kns, vns, *lams, *page_args)


def _sample_mix(qdp, qsp, kd_rows, vd_rows, kst, vst, n_seq, t_new, lams, lambda_init, caches, page_table, layer):
    q5 = qdp.astype(F32).reshape(n_seq, t_new, N_DIFF_HEADS, 2, HEAD_DIM).transpose(0, 2, 3, 1, 4)
    zero = jnp.zeros_like(q5[:, :, 0])
    q0 = jnp.concatenate([q5[:, :, 0], zero], axis=-1)
    q1 = jnp.concatenate([zero, q5[:, :, 1]], axis=-1)
    qd_rows = jnp.concatenate([q0, q1], axis=2)
    qs4 = qsp.astype(F32).reshape(n_seq, t_new, N_SB_HEADS, HEAD_DIM).transpose(0, 2, 1, 3)
    qs_rows = jnp.pad(qs4, ((0, 0), (0, 0), (0, SUBLANES - t_new), (0, 0)))

    def pad_rows(a):
        a = a.reshape(n_seq, t_new * N_DIFF_HEADS, LANES)
        return jnp.pad(a, ((0, 0), (0, (NEW_PAD - t_new) * N_DIFF_HEADS), (0, 0)))

    def pad_tr(a):
        a = a.reshape(N_SB_HEADS, HEAD_DIM, n_seq, t_new).transpose(2, 0, 1, 3)
        return jnp.pad(a, ((0, 0), (0, 0), (0, 0), (0, NEW_PAD - t_new)))

    od, os_ = _sample_attention(
        qd_rows, qs_rows, pad_rows(kd_rows), pad_rows(vd_rows), pad_tr(kst), pad_tr(vst),
        lams, lambda_init, t_new, *caches, page_table, layer)
    od = od.transpose(0, 2, 1, 3).reshape(n_seq * t_new, GROUP)
    os_ = os_[:, :, :t_new].transpose(0, 2, 1, 3).reshape(n_seq * t_new, GROUP)
    return od, os_


MEM_HALVES = MEM_HEAD_DIM // LANES
MEM_ROWS = MEM_HALVES * N_MEM_HEADS


def _mem_kv_kernel(mem_ref, wk_ref, wv_ref, mk_ref, mv_ref):
    m = mem_ref[...].astype(BF16)
    n_tok = m.shape[0]
    for w_ref, o_ref in ((wk_ref, mk_ref), (wv_ref, mv_ref)):
        y = jnp.dot(m, w_ref[...], preferred_element_type=F32)
        for half in range(MEM_HALVES):
            for h in range(N_MEM_HEADS):
                col = h * MEM_HEAD_DIM + half * LANES
                o_ref[pl.ds(half * N_MEM_HEADS + h, n_tok, stride=MEM_ROWS), :] = y[:, col:col + LANES]


def _mem_kv(mem2d, w_ck, w_cv):
    n, d = mem2d.shape
    tm = min(256, n)
    assert n % tm == 0
    whole = pl.BlockSpec((d, d), lambda i: (0, 0))
    out = jax.ShapeDtypeStruct((n * MEM_ROWS, LANES), F32)
    ospec = pl.BlockSpec((tm * MEM_ROWS, LANES), lambda i: (i, 0))
    return pl.pallas_call(
        _mem_kv_kernel,
        grid=(n // tm,),
        in_specs=[pl.BlockSpec((tm, d), lambda i: (i, 0)), whole, whole],
        out_specs=[ospec, ospec],
        out_shape=[out, out],
        compiler_params=_params("parallel"),
        name="mem_kv",
    )(mem2d, w_ck.astype(BF16), w_cv.astype(BF16))


def _layer_norm(x, g, b):
    mu = jnp.mean(x, axis=-1, keepdims=True)
    xc = x - mu
    var = jnp.mean(xc * xc, axis=-1, keepdims=True)
    return xc * lax.rsqrt(var + LN_EPS) * g + b


def _rms_heads(o, g, width):
    lane = lax.broadcasted_iota(jnp.int32, (o.shape[0], LANES), 1)
    outs = []
    for j in range(o.shape[1] // LANES):
        seg = o[:, j * LANES:(j + 1) * LANES]
        sq = seg * seg
        if width == LANES:
            ms = jnp.sum(sq, axis=-1, keepdims=True) * (1.0 / width)
        else:
            lo = lane < width
            ms_lo = jnp.sum(jnp.where(lo, sq, 0.0), axis=-1, keepdims=True)
            ms_hi = jnp.sum(jnp.where(lo, 0.0, sq), axis=-1, keepdims=True)
            ms = jnp.where(lo, ms_lo, ms_hi) * (1.0 / width)
        outs.append(seg * lax.rsqrt(ms + RMS_EPS) * g[:, j * LANES:(j + 1) * LANES])
    return jnp.concatenate(outs, axis=1)


def _post_attn_kernel(x_ref, od_ref, os_ref, gd_ref, gs_ref, wo_ref, g1_ref, b1_ref, wq_ref,
                      mk_ref, mv_ref, wc_ref, g2_ref, b2_ref, x2_ref, x2b_ref, *, groups, lambda_init):
    tm = x_ref.shape[0]
    rows = tm // groups
    od = _rms_heads(od_ref[...], gd_ref[...], 2 * HEAD_DIM) * (1.0 - lambda_init)
    os_ = _rms_heads(os_ref[...], gs_ref[...], HEAD_DIM)
    mix = jnp.concatenate([od, os_], axis=1).astype(BF16)
    h1 = DEEPNORM_ALPHA * x_ref[...] + jnp.dot(mix, wo_ref[...], preferred_element_type=F32)
    x1 = _layer_norm(h1, g1_ref[...], b1_ref[...])
    q = (jnp.dot(x1.astype(BF16), wq_ref[...], preferred_element_type=F32) * MEM_HEAD_DIM ** -0.5).astype(BF16)

    n_mem = mk_ref.shape[1] // MEM_ROWS
    row_group = lax.broadcasted_iota(jnp.int32, (tm, LANES), 0) // rows
    head_outs = []
    for h in range(N_MEM_HEADS):
        halves = [None] * MEM_HALVES
        for g in range(groups):
            def part(ref, half):
                return ref[g, pl.ds(half * N_MEM_HEADS + h, n_mem, stride=MEM_ROWS), :].astype(BF16)
            s = sum(_dot_nt(q[:, h * MEM_HEAD_DIM + half * LANES: h * MEM_HEAD_DIM + (half + 1) * LANES],
                            part(mk_ref, half)) for half in range(MEM_HALVES))
            p = jnp.exp(s - jnp.max(s, axis=-1, keepdims=True))
            p = (p / jnp.sum(p, axis=-1, keepdims=True)).astype(BF16)
            for half in range(MEM_HALVES):
                o = jnp.dot(p, part(mv_ref, half), preferred_element_type=F32)
                halves[half] = o if groups == 1 else jnp.where(
                    row_group == g, o, 0.0 if halves[half] is None else halves[half])
        head_outs.extend(halves)
    o_mem = jnp.concatenate(head_outs, axis=1).astype(BF16)
    h2 = DEEPNORM_ALPHA * x1 + jnp.dot(o_mem, wc_ref[...], preferred_element_type=F32)
    x2 = _layer_norm(h2, g2_ref[...], b2_ref[...])
    x2_ref[...] = x2
    x2b_ref[...] = x2.astype(BF16)


def _post_attention(x2d, od, os_, mem_k, mem_v, w, lambda_init, tokens_per_group):
    n, d = x2d.shape
    if tokens_per_group >= 512:
        tm, groups = 512, 1
        assert tokens_per_group % tm == 0
        mem_idx = lambda i: (i * tm // tokens_per_group, 0, 0)
    else:
        groups = 4
        tm = groups * tokens_per_group
        mem_idx = lambda i: (i, 0, 0)
    assert n % tm == 0
    tok = lambda wd: pl.BlockSpec((tm, wd), lambda i: (i, 0))
    vec = lambda wd: pl.BlockSpec((1, wd), lambda i: (0, 0))
    mat = pl.BlockSpec((d, d), lambda i: (0, 0))
    mem = pl.BlockSpec((groups,) + mem_k.shape[1:], mem_idx)
    return pl.pallas_call(
        functools.partial(_post_attn_kernel, groups=groups, lambda_init=lambda_init),
        grid=(n // tm,),
        in_specs=[tok(d), tok(GROUP), tok(GROUP), vec(GROUP), vec(GROUP), mat, vec(d), vec(d), mat,
                  mem, mem, mat, vec(d), vec(d)],
        out_specs=[tok(d), tok(d)],
        out_shape=[jax.ShapeDtypeStruct((n, d), F32), jax.ShapeDtypeStruct((n, d), BF16)],
        compiler_params=_params("parallel"),
        name="post_attn",
    )(x2d, od, os_, w["g_diff"], w["g_sb"], w["w_out"], w["ln1_g"], w["ln1_b"], w["w_cq"],
      mem_k, mem_v, w["w_co"], w["ln2_g"], w["ln2_b"])


MOE_BLOCK = 256


def _router_kernel(x_ref, wr_ref, br_ref, e_ref, gate_ref, rank_ref, count_ref, carry_scr):
    i = pl.program_id(0)
    tm = x_ref.shape[0]

    @pl.when(i == 0)
    def _():
        carry_scr[...] = jnp.zeros(carry_scr.shape, F32)

    logits = lax.dot_general(wr_ref[...], x_ref[...], (((1,), (1,)), ((), ())),
                             precision=lax.Precision.HIGHEST, preferred_element_type=F32) + br_ref[...]
    expert = lax.broadcasted_iota(jnp.int32, logits.shape, 0)
    member = jnp.zeros(logits.shape, F32)
    picks, tops = [], []
    work = logits
    for _ in range(TOP_K):
        top = jnp.max(work, axis=0, keepdims=True)
        idx = jnp.min(jnp.where(work == top, expert, N_EXPERTS), axis=0, keepdims=True)
        hit = expert == idx
        member = jnp.where(hit, 1.0, member)
        work = jnp.where(hit, -jnp.inf, work)
        picks.append(idx)
        tops.append(top)
    weights = [jnp.exp(t - tops[0]) for t in tops]
    total = sum(weights)
    before = (lax.broadcasted_iota(jnp.int32, (tm, tm), 0)
              < lax.broadcasted_iota(jnp.int32, (tm, tm), 1)).astype(BF16)
    prefix = jnp.dot(member.astype(BF16), before, preferred_element_type=F32) + carry_scr[:, 0:1]
    for k in range(TOP_K):
        e_ref[k:k + 1, :] = picks[k]
        gate_ref[k:k + 1, :] = weights[k] / total
        rank_ref[k:k + 1, :] = jnp.sum(jnp.where(expert == picks[k], prefix, 0.0), axis=0,
                                       keepdims=True).astype(jnp.int32)
    carry_scr[...] = carry_scr[...] + jnp.sum(member, axis=1, keepdims=True)
    count_ref[...] = carry_scr[...].astype(jnp.int32)


def _route(x2, w_router, b_router):
    n, d = x2.shape
    tm = 512
    assert n % tm == 0
    out = lambda dt: jax.ShapeDtypeStruct((TOP_K, n), dt)
    ospec = pl.BlockSpec((TOP_K, tm), lambda i: (0, i))
    e, gate, rank, count = pl.pallas_call(
        _router_kernel,
        grid=(n // tm,),
        in_specs=[pl.BlockSpec((tm, d), lambda i: (i, 0)),
                  pl.BlockSpec((N_EXPERTS, d), lambda i: (0, 0)),
                  pl.BlockSpec((N_EXPERTS, 1), lambda i: (0, 0))],
        out_specs=[ospec, ospec, ospec, pl.BlockSpec((N_EXPERTS, LANES), lambda i: (0, 0))],
        out_shape=[out(jnp.int32), out(F32), out(jnp.int32), jax.ShapeDtypeStruct((N_EXPERTS, LANES), jnp.int32)],
        scratch_shapes=[pltpu.VMEM((N_EXPERTS, LANES), F32)],
        compiler_params=_params("arbitrary"),
        name="moe_router",
    )(x2, w_router.T, b_router.reshape(N_EXPERTS, 1))
    return e, gate, rank, count[:, 0]


def _moe_plan(e, rank, count, n_blocks):
    padded = (count + MOE_BLOCK - 1) // MOE_BLOCK * MOE_BLOCK
    pad_end = jnp.cumsum(padded)
    dest = (pad_end - padded)[e] + rank
    block_e = jnp.minimum(jnp.searchsorted(pad_end, jnp.arange(n_blocks) * MOE_BLOCK, side="right"),
                          N_EXPERTS - 1).astype(jnp.int32)
    return dest.astype(jnp.int32), block_e


def _row_copies(src_at, dst_at, sem, n_tok, unroll=8):
    def each(fn):
        def body(r, carry):
            for k in range(TOP_K):
                fn(pltpu.make_async_copy(src_at(r, k), dst_at(r, k), sem))
            return carry
        lax.fori_loop(0, n_tok, body, 0, unroll=unroll)

    each(lambda c: c.start())
    each(lambda c: c.wait())


def _dispatch_kernel(dest_ref, x_ref, xs_in_ref, xs_ref, sem):
    del xs_in_ref
    _row_copies(lambda r, k: x_ref.at[pl.ds(r, 1), :],
                lambda r, k: xs_ref.at[pl.ds(dest_ref[k, r], 1), :], sem, x_ref.shape[0])


def _dispatch(x2, dest, n_rows):
    n, d = x2.shape
    tm = 256
    assert n % tm == 0
    return pl.pallas_call(
        _dispatch_kernel,
        grid=(n // tm,),
        in_specs=[pl.BlockSpec((TOP_K, tm), lambda i: (0, i), memory_space=pltpu.SMEM),
                  pl.BlockSpec((tm, d), lambda i: (i, 0)),
                  pl.BlockSpec(memory_space=pl.ANY)],
        out_specs=pl.BlockSpec(memory_space=pl.ANY),
        out_shape=jax.ShapeDtypeStruct((n_rows, d), x2.dtype),
        scratch_shapes=[pltpu.SemaphoreType.DMA(())],
        input_output_aliases={2: 0},
        compiler_params=_params("arbitrary"),
        name="moe_dispatch",
    )(dest, x2, jnp.zeros((n_rows, d), x2.dtype))


def _expert_kernel(be_ref, x_ref, wg_ref, bg_ref, wu_ref, bu_ref, wd_ref, bd_ref, y_ref):
    del be_ref
    x = x_ref[...].astype(BF16)
    g = jnp.minimum(jnp.dot(x, wg_ref[...], preferred_element_type=F32) + bg_ref[...], SWIGLU_LIMIT)
    u = jnp.clip(jnp.dot(x, wu_ref[...], preferred_element_type=F32) + bu_ref[...], -SWIGLU_LIMIT, SWIGLU_LIMIT)
    h = (u + 1.0) * g * jax.nn.sigmoid(SWIGLU_ALPHA * g)
    y_ref[...] = jnp.dot(h.astype(BF16), wd_ref[...], preferred_element_type=F32) + bd_ref[...]


def _experts(xs, block_e, w_gate, b_gate, w_up, b_up, w_down, b_down):
    n_rows, d = xs.shape
    f = w_gate.shape[-1]
    rows = pl.BlockSpec((MOE_BLOCK, d), lambda i, be: (i, 0))
    wspec = lambda a, b: pl.BlockSpec((None, a, b), lambda i, be: (be[i], 0, 0))
    return pl.pallas_call(
        _expert_kernel,
        grid_spec=pltpu.PrefetchScalarGridSpec(
            num_scalar_prefetch=1,
            grid=(n_rows // MOE_BLOCK,),
            in_specs=[rows, wspec(d, f), wspec(1, f), wspec(d, f), wspec(1, f), wspec(f, d), wspec(1, d)],
            out_specs=rows),
        out_shape=jax.ShapeDtypeStruct((n_rows, d), F32),
        compiler_params=_params("arbitrary"),
        name="moe_experts",
    )(block_e, xs, w_gate, b_gate, w_up, b_up, w_down, b_down)


def _combine_kernel(dest_ref, x2_ref, gate_ref, g3_ref, b3_ref, yb_ref, out_ref, buf, sem):
    tm = x2_ref.shape[0]
    _row_copies(lambda r, k: yb_ref.at[pl.ds(dest_ref[k, r], 1), :],
                lambda r, k: buf.at[k, pl.ds(r, 1), :], sem, tm)
    gate = gate_ref[...]
    y = sum(gate[:, k:k + 1] * buf[k] for k in range(TOP_K))
    out_ref[...] = _layer_norm(DEEPNORM_ALPHA * x2_ref[...] + y, g3_ref[...], b3_ref[...])


def _combine(x2, gate_t, dest, yb, ln3_g, ln3_b):
    n, d = x2.shape
    tm = 128
    assert n % tm == 0
    vec = pl.BlockSpec((1, d), lambda i: (0, 0))
    return pl.pallas_call(
        _combine_kernel,
        grid=(n // tm,),
        in_specs=[pl.BlockSpec((TOP_K, tm), lambda i: (0, i), memory_space=pltpu.SMEM),
                  pl.BlockSpec((tm, d), lambda i: (i, 0)),
                  pl.BlockSpec((tm, TOP_K), lambda i: (i, 0)),
                  vec, vec, pl.BlockSpec(memory_space=pl.ANY)],
        out_specs=pl.BlockSpec((tm, d), lambda i: (i, 0)),
        out_shape=jax.ShapeDtypeStruct((n, d), F32),
        scratch_shapes=[pltpu.VMEM((TOP_K, tm, d), F32), pltpu.SemaphoreType.DMA(())],
        compiler_params=_params("arbitrary"),
        name="moe_combine",
    )(dest, x2, gate_t, ln3_g, ln3_b, yb)


def _moe(x2, w):
    n, d = x2.shape
    e, gate, rank, count = _route(x2, w["w_router"], w["b_router"])
    n_blocks = (n * TOP_K + N_EXPERTS * (MOE_BLOCK - 1) + MOE_BLOCK - 1) // MOE_BLOCK
    dest, block_e = _moe_plan(e, rank, count, n_blocks)
    xs = _dispatch(x2, dest, n_blocks * MOE_BLOCK)
    yb = _experts(xs, block_e, w["w_gate"], w["b_gate"], w["w_up"], w["b_up"], w["w_down"], w["b_down"])
    return _combine(x2, gate.T, dest, yb, w["ln3_g"], w["ln3_b"])


def kernel(x_prompt, x_sample, mem_prompt, cache_diff_k, cache_diff_v, cache_sb_k, cache_sb_v, cache_mem_k, cache_mem_v, page_table, w_in, lambda_q1, lambda_k1, lambda_q2, lambda_k2, g_diff, g_sb, w_out, ln1_g, ln1_b, w_cq, w_ck, w_cv, w_co, ln2_g, ln2_b, w_router, b_router, w_gate, b_gate, w_up, b_up, w_down, b_down, ln3_g, ln3_b):
    assert w_in.shape[0] == DEPTH
    layer = 0
    lambda_init = 0.8 - 0.6 * math.exp(-0.3 * layer)
    b, t, d = x_prompt.shape
    n_seq, t_new, _ = x_sample.shape
    n_mem = mem_prompt.shape[1]
    n_p, n_s = b * t, n_seq * t_new
    past = page_table.shape[1] * PAGE_SIZE
    lams = [lambda_q1, lambda_k1, lambda_q2, lambda_k2]
    bf = lambda a: a[layer].astype(BF16)
    w = {"g_diff": g_diff, "g_sb": g_sb, "w_out": bf(w_out), "ln1_g": ln1_g, "ln1_b": ln1_b,
         "w_cq": bf(w_cq), "w_co": bf(w_co), "ln2_g": ln2_g, "ln2_b": ln2_b,
         "w_router": w_router[layer], "b_router": b_router[layer],
         "w_gate": bf(w_gate), "b_gate": b_gate[layer][:, None, :], "w_up": bf(w_up), "b_up": b_up[layer][:, None, :],
         "w_down": bf(w_down), "b_down": b_down[layer][:, None, :], "ln3_g": ln3_g, "ln3_b": ln3_b}

    def mem_rows(m):
        g = m.shape[0]
        m = m.reshape(g, n_mem, N_MEM_HEADS, MEM_HALVES, LANES).transpose(0, 1, 3, 2, 4)
        return m.reshape(g, n_mem * MEM_ROWS, LANES)

    def mem_state(rows, g):
        m = rows.reshape(g, n_mem, MEM_HALVES, N_MEM_HEADS, LANES).transpose(0, 1, 3, 2, 4)
        return m.reshape(1, g, n_mem, N_MEM_HEADS, MEM_HEAD_DIM)

    qd, _, kd_p, kdb, vd_p, vdb, qs, _, kst_p, kstb, vst_p, vstb = _project(
        x_prompt, jnp.tile(jnp.arange(t), b), w_in[layer])
    od_p, os_p = _prompt_attention(qd, kdb, vdb, qs, kstb, vstb, lams, lambda_init)
    mk_p, mv_p = _mem_kv(mem_prompt.reshape(b * n_mem, d), w_ck[layer], w_cv[layer])
    x2_p, _ = _post_attention(x_prompt.reshape(n_p, d), od_p.reshape(n_p, GROUP), os_p.reshape(n_p, GROUP),
                              mk_p.reshape(b, n_mem * MEM_ROWS, LANES), mv_p.reshape(b, n_mem * MEM_ROWS, LANES),
                              w, lambda_init, t)

    _, qdp, kd_s, _, vd_s, _, _, qsp, kst_s, _, vst_s, _ = _project(
        x_sample.reshape(1, n_s, d), jnp.tile(past + jnp.arange(t_new), n_seq), w_in[layer])
    od_s, os_s = _sample_mix(qdp[0], qsp[0], kd_s, vd_s, kst_s[0], vst_s[0], n_seq, t_new, lams, lambda_init,
                             (cache_diff_k, cache_diff_v, cache_sb_k, cache_sb_v), page_table, layer)
    x2_s, _ = _post_attention(x_sample.reshape(n_s, d), od_s, os_s,
                              mem_rows(cache_mem_k[layer]), mem_rows(cache_mem_v[layer]), w, lambda_init, t_new)

    y = _moe(jnp.concatenate([x2_p, x2_s], axis=0), w)
    y_p, y_s = y[:n_p].reshape(b, t, d), y[n_p:].reshape(n_seq, t_new, d)

    diff_state = lambda rows, g, tt: rows.reshape(1, g, tt, N_DIFF_HEADS, 2 * HEAD_DIM)
    sb_state_p = lambda tr: tr.reshape(b, N_SB_HEADS, HEAD_DIM, t).transpose(0, 3, 1, 2)[None]
    sb_state_s = lambda tr: tr.reshape(N_SB_HEADS, HEAD_DIM, n_seq, t_new).transpose(2, 3, 0, 1)[None]
    return (y_p, y_s,
            diff_state(kd_p, b, t), diff_state(vd_p, b, t), sb_state_p(kst_p), sb_state_p(vst_p),
            mem_state(mk_p, b), mem_state(mv_p, b),
            diff_state(kd_s, n_seq, t_new), diff_state(vd_s, n_seq, t_new),
            sb_state_s(kst_s[0]), sb_state_s(vst_s[0]))
```
